```python
import math
import jax, jax.numpy as jnp
from jax import lax
import numpy as np

D_MODEL = 1024
BATCH = 8
SEQ = 4096
DEPTH = 4

GRID_W = 64
CTX_LEN = 256
HEAD_DIM = 64
N_Q_HEADS = 8
N_KV_HEADS = 2
Q_PER_KV = N_Q_HEADS // N_KV_HEADS
D_ATTN = N_Q_HEADS * HEAD_DIM
D_KV = N_KV_HEADS * HEAD_DIM
WINDOW = 128
BLOCK = 128
ROPE_THETA = 10000.0
D_SSM = D_MODEL // 2
SSM_GROUP = 16
N_SSM_GROUPS = D_SSM // SSM_GROUP
SSM_STATE = 64
D_IN = D_ATTN + 2 * D_KV + D_SSM + 2 * D_MODEL
D_FF = 2816
N_EXPERTS = 8
TOP_K = 2
D_FF_EXPERT = 3584
N_DENSE = (DEPTH + 1) // 2
N_MOE = DEPTH // 2
RMS_EPS = 1e-6
NEG_INF = -1e30

kernel_name = 'hybrid_dit_window_gqa_s5_moe'


def _rmsnorm(x, g):
    xf = x.astype(jnp.float32)
    y = xf * lax.rsqrt(jnp.mean(xf * xf, axis=-1, keepdims=True) + RMS_EPS)
    return (y * g.astype(jnp.float32)).astype(x.dtype)


def _modulate(h, shift, scale):
    return h * (1.0 + scale) + shift


def _split_proj(p):
    sizes = (D_ATTN, D_KV, D_KV, D_SSM, D_MODEL, D_MODEL)
    idx = [sum(sizes[:i + 1]) for i in range(len(sizes) - 1)]
    return jnp.split(p, idx, axis=-1)


def _heads(t, n):
    return t.reshape(t.shape[:2] + (n, HEAD_DIM))


def _axial_rope(L):
    rows = L // GRID_W
    row = jnp.broadcast_to(jnp.arange(rows, dtype=jnp.float32)[:, None], (rows, GRID_W)).reshape(L)
    col = jnp.broadcast_to(jnp.arange(GRID_W, dtype=jnp.float32)[None, :], (rows, GRID_W)).reshape(L)
    n_freq = HEAD_DIM // 4
    inv_freq = ROPE_THETA ** (-jnp.arange(n_freq, dtype=jnp.float32) / n_freq)
    ang = jnp.concatenate([row[:, None] * inv_freq, col[:, None] * inv_freq], axis=-1)
    return jnp.cos(ang), jnp.sin(ang)


def _rope(x, cos, sin):
    half = HEAD_DIM // 2
    x1, x2 = x[..., :half], x[..., half:]
    cs = cos[:, None, :].astype(x.dtype)
    sn = sin[:, None, :].astype(x.dtype)
    return jnp.concatenate([x1 * cs - x2 * sn, x2 * cs + x1 * sn], axis=-1)


def _latent_window_attention(q, k, v, kc, vc, sink):
    Bn, L = q.shape[:2]
    C = kc.shape[1]
    nb = L // BLOCK
    scale = HEAD_DIM ** -0.5
    qb = q.reshape(Bn, nb, BLOCK, N_KV_HEADS, Q_PER_KV, HEAD_DIM)
    pad = ((0, 0), (BLOCK, BLOCK), (0, 0), (0, 0))
    kp = jnp.pad(k, pad).reshape(Bn, nb + 2, BLOCK, N_KV_HEADS, HEAD_DIM)
    vp = jnp.pad(v, pad).reshape(Bn, nb + 2, BLOCK, N_KV_HEADS, HEAD_DIM)
    kb = jnp.concatenate([kp[:, :-2], kp[:, 1:-1], kp[:, 2:]], axis=2)
    vb = jnp.concatenate([vp[:, :-2], vp[:, 1:-1], vp[:, 2:]], axis=2)
    qpos = jnp.arange(nb)[:, None, None] * BLOCK + jnp.arange(BLOCK)[None, :, None]
    kpos = jnp.arange(nb)[:, None, None] * BLOCK - BLOCK + jnp.arange(3 * BLOCK)[None, None, :]
    valid = (jnp.abs(qpos - kpos) <= WINDOW) & (kpos >= 0) & (kpos < L)
    s_loc = jnp.einsum('bnqhgd,bnkhd->bnhgqk', qb, kb, preferred_element_type=jnp.float32) * scale
    s_loc = jnp.where(valid[:, None, None], s_loc, NEG_INF)
    s_ctx = jnp.einsum('bnqhgd,bchd->bnhgqc', qb, kc, preferred_element_type=jnp.float32) * scale
    s_sink = jnp.broadcast_to(sink.astype(jnp.float32).reshape(N_KV_HEADS, Q_PER_KV, 1, 1),
                              s_loc.shape[:-1] + (1,))
    p = jax.nn.softmax(jnp.concatenate([s_loc, s_ctx, s_sink], axis=-1), axis=-1)
    p_loc = p[..., :3 * BLOCK].astype(v.dtype)
    p_ctx = p[..., 3 * BLOCK:3 * BLOCK + C].astype(v.dtype)
    o = (jnp.einsum('bnhgqk,bnkhd->bnqhgd', p_loc, vb)
         + jnp.einsum('bnhgqc,bchd->bnqhgd', p_ctx, vc))
    return o.reshape(Bn, L, D_ATTN)


def _context_attention(q, k, v, sink):
    Bn, C = q.shape[:2]
    scale = HEAD_DIM ** -0.5
    qg = q.reshape(Bn, C, N_KV_HEADS, Q_PER_KV, HEAD_DIM)
    s = jnp.einsum('bqhgd,bkhd->bhgqk', qg, k, preferred_element_type=jnp.float32) * scale
    s_sink = jnp.broadcast_to(sink.astype(jnp.float32).reshape(N_KV_HEADS, Q_PER_KV, 1, 1),
                              s.shape[:-1] + (1,))
    p = jax.nn.softmax(jnp.concatenate([s, s_sink], axis=-1), axis=-1)[..., :C]
    o = jnp.einsum('bhgqk,bkhd->bqhgd', p.astype(v.dtype), v)
    return o.reshape(Bn, C, D_ATTN)


def _zoh(a_re, a_im, log_dt, b_re, b_im):
    a = lax.complex(a_re.astype(jnp.float32), a_im.astype(jnp.float32))
    dt = jnp.exp(log_dt.astype(jnp.float32))[..., None]
    a_bar = jnp.exp(a * dt)
    b = lax.complex(b_re.astype(jnp.float32), b_im.astype(jnp.float32))
    b_bar = ((a_bar - 1.0) / a)[..., None] * b
    return a_bar, b_bar


def _scan_combine(e1, e2):
    return (e1[0] * e2[0], e2[0] * e1[1] + e2[1])


def _diag_scan(u, a_bar, b_bar, s0, reverse):
    L = u.shape[1]
    bu = jnp.einsum('blgh,gph->blgp', u.astype(jnp.complex64), b_bar)
    a = jnp.broadcast_to(a_bar, (1, L) + a_bar.shape)
    a_cum, s = lax.associative_scan(_scan_combine, (a, bu), reverse=reverse, axis=1)
    if s0 is not None:
        s = s + a_cum * s0[:, None]
    return s


def _readout(s, c_mat):
    return jnp.einsum('blgp,ghp->blgh', s, c_mat).real


def _bidir_ssm(u_lat, u_ctx, a_bar, b_bar, c_mat, d_skip, need_ctx_out):
    Bn, L, _ = u_lat.shape
    C = u_ctx.shape[1]
    ul = u_lat.astype(jnp.float32).reshape(Bn, L, N_SSM_GROUPS, SSM_GROUP)
    uc = u_ctx.astype(jnp.float32).reshape(Bn, C, N_SSM_GROUPS, SSM_GROUP)
    d = d_skip.astype(jnp.float32)
    sc_f = _diag_scan(uc, a_bar[0], b_bar[0], None, False)
    sc_b = _diag_scan(uc, a_bar[1], b_bar[1], None, True)
    sl_f = _diag_scan(ul, a_bar[0], b_bar[0], sc_f[:, -1], False)
    sl_b = _diag_scan(ul, a_bar[1], b_bar[1], sc_b[:, 0], True)
    y_lat = ((_readout(sl_f, c_mat[0]) + _readout(sl_b, c_mat[1])).reshape(Bn, L, D_SSM)
             + d * u_lat.astype(jnp.float32))
    y_ctx = None
    if need_ctx_out:
        y_ctx = ((_readout(sc_f, c_mat[0]) + _readout(sc_b, c_mat[1])).reshape(Bn, C, D_SSM)
                 + d * u_ctx.astype(jnp.float32))
        y_ctx = y_ctx.astype(u_ctx.dtype)
    return y_lat.astype(u_lat.dtype), y_ctx


def _glu(y, w, b):
    g = jax.nn.gelu(y)
    return g * jax.nn.sigmoid(g @ w + b)


def _merge(o_attn, y_ssm, g_attn, g_ssm, w_pa, w_pb, w_o):
    m = jax.nn.sigmoid(g_attn) * (o_attn @ w_pa) + jax.nn.sigmoid(g_ssm) * (y_ssm @ w_pb)
    return m @ w_o


def _swiglu(h, wg, wu, wd):
    return (jax.nn.silu(h @ wg) * (h @ wu)) @ wd


def _moe_swiglu(h, w_router, b_router, wg, wu, wd):
    shp = h.shape
    hf = h.reshape(-1, shp[-1])
    logits = (hf @ w_router).astype(jnp.float32) + b_router.astype(jnp.float32)
    top_v, top_i = lax.top_k(logits, TOP_K)
    top_w = jax.nn.softmax(top_v, axis=-1)
    gates = jnp.sum(jax.nn.one_hot(top_i, N_EXPERTS, dtype=jnp.float32) * top_w[..., None], axis=-2)
    gates = gates.astype(h.dtype)
    y = jnp.zeros_like(hf)
    for e in range(N_EXPERTS):
        y = y + gates[:, e:e + 1] * _swiglu(hf, wg[e], wu[e], wd[e])
    return y.reshape(shp)


def setup_inputs(seed: int = 0) -> dict:
    key = jax.random.key(seed)
    ks = iter(jax.random.split(key, 40))
    f32 = jnp.float32

    def nrm(shape, s):
        return s * jax.random.normal(next(ks), shape, f32)

    D = D_MODEL
    G, P, H = N_SSM_GROUPS, SSM_STATE, SSM_GROUP
    n_idx = jnp.arange(P, dtype=f32)
    return {
        'x': nrm((BATCH, SEQ, D), 1.0),
        'c': nrm((BATCH, D), 1.0),
        'ctx': nrm((BATCH, CTX_LEN, D), 1.0),
        'c_ctx': nrm((D,), 1.0),
        'w_mod': nrm((DEPTH, D, 6 * D), 0.5 * D ** -0.5),
        'b_mod': nrm((DEPTH, 6 * D), 0.02),
        'norm_g': 1.0 + nrm((DEPTH, 4, D), 0.02),
        'w_in': nrm((DEPTH, D, D_IN), D ** -0.5),
        'attn_sink': nrm((DEPTH, N_Q_HEADS), 0.5),
        'ssm_a_re': -0.5 + nrm((DEPTH, 2, G, P), 0.01),
        'ssm_a_im': math.pi * n_idx + nrm((DEPTH, 2, G, P), 0.01),
        'ssm_log_dt': jax.random.uniform(next(ks), (DEPTH, 2, G), f32, math.log(1e-3), math.log(1e-1)),
        'ssm_b_re': nrm((DEPTH, 2, G, P, H), (2.0 * H) ** -0.5),
        'ssm_b_im': nrm((DEPTH, 2, G, P, H), (2.0 * H) ** -0.5),
        'ssm_c_re': nrm((DEPTH, 2, G, H, P), (2.0 * P) ** -0.5),
        'ssm_c_im': nrm((DEPTH, 2, G, H, P), (2.0 * P) ** -0.5),
        'ssm_d': nrm((DEPTH, D_SSM), 1.0),
        'w_glu': nrm((DEPTH, D_SSM, D_SSM), D_SSM ** -0.5),
        'b_glu': nrm((DEPTH, D_SSM), 0.02),
        'w_branch_attn': nrm((DEPTH, D_ATTN, D), D_ATTN ** -0.5),
        'w_branch_ssm': nrm((DEPTH, D_SSM, D), D_SSM ** -0.5),
        'w_out': nrm((DEPTH, D, D), D ** -0.5),
        'w_ffn_gate': nrm((N_DENSE, D, D_FF), D ** -0.5),
        'w_ffn_up': nrm((N_DENSE, D, D_FF), D ** -0.5),
        'w_ffn_down': nrm((N_DENSE, D_FF, D), D_FF ** -0.5),
        'w_router': nrm((N_MOE, D, N_EXPERTS), D ** -0.5),
        'b_router': nrm((N_MOE, N_EXPERTS), 0.01),
        'w_exp_gate': nrm((N_MOE, N_EXPERTS, D, D_FF_EXPERT), D ** -0.5),
        'w_exp_up': nrm((N_MOE, N_EXPERTS, D, D_FF_EXPERT), D ** -0.5),
        'w_exp_down': nrm((N_MOE, N_EXPERTS, D_FF_EXPERT, D), D_FF_EXPERT ** -0.5),
    }


def reference(x, c, ctx, c_ctx, w_mod, b_mod, norm_g, w_in, attn_sink, ssm_a_re, ssm_a_im,
              ssm_log_dt, ssm_b_re, ssm_b_im, ssm_c_re, ssm_c_im, ssm_d, w_glu, b_glu,
              w_branch_attn, w_branch_ssm, w_out, w_ffn_gate, w_ffn_up, w_ffn_down,
              w_router, b_router, w_exp_gate, w_exp_up, w_exp_down):
    L = x.shape[1]
    cos, sin = _axial_rope(L)
    xc = ctx
    cond_lat = jax.nn.silu(c)
    cond_ctx = jax.nn.silu(c_ctx)
    for l in range(DEPTH):
        last = l == DEPTH - 1
        mod_l = jnp.split((cond_lat @ w_mod[l] + b_mod[l])[:, None, :], 6, axis=-1)
        mod_c = jnp.split(cond_ctx @ w_mod[l] + b_mod[l], 6, axis=-1)

        h_l = _modulate(_rmsnorm(x, norm_g[l, 0]), mod_l[0], mod_l[1])
        h_c = _modulate(_rmsnorm(xc, norm_g[l, 0]), mod_c[0], mod_c[1])
        q_l, k_l, v_l, u_l, ga_l, gb_l = _split_proj(h_l @ w_in[l])
        q_c, k_c, v_c, u_c, ga_c, gb_c = _split_proj(h_c @ w_in[l])
        k_c = _heads(k_c, N_KV_HEADS)
        v_c = _heads(v_c, N_KV_HEADS)

        q_lh = _rope(_heads(q_l, N_Q_HEADS), cos, sin)
        k_lh = _rope(_heads(k_l, N_KV_HEADS), cos, sin)
        o_l = _latent_window_attention(q_lh, k_lh, _heads(v_l, N_KV_HEADS), k_c, v_c, attn_sink[l])

        a_bar, b_bar = _zoh(ssm_a_re[l], ssm_a_im[l], ssm_log_dt[l], ssm_b_re[l], ssm_b_im[l])
        c_mat = lax.complex(ssm_c_re[l].astype(jnp.float32), ssm_c_im[l].astype(jnp.float32))
        y_l, y_c = _bidir_ssm(u_l, u_c, a_bar, b_bar, c_mat, ssm_d[l], not last)
        s_l = _glu(y_l, w_glu[l], b_glu[l])

        m_l = _merge(o_l, s_l, ga_l, gb_l, w_branch_attn[l], w_branch_ssm[l], w_out[l])
        x = x + mod_l[2] * _rmsnorm(m_l, norm_g[l, 1])
        if not last:
            o_c = _context_attention(_heads(q_c, N_Q_HEADS), k_c, v_c, attn_sink[l])
            s_c = _glu(y_c, w_glu[l], b_glu[l])
            m_c = _merge(o_c, s_c, ga_c, gb_c, w_branch_attn[l], w_branch_ssm[l], w_out[l])
            xc = xc + mod_c[2] * _rmsnorm(m_c, norm_g[l, 1])

        j = l // 2
        if l % 2 == 0:
            ffn = lambda h: _swiglu(h, w_ffn_gate[j], w_ffn_up[j], w_ffn_down[j])
        else:
            ffn = lambda h: _moe_swiglu(h, w_router[j], b_router[j], w_exp_gate[j], w_exp_up[j], w_exp_down[j])
        h_l = _modulate(_rmsnorm(x, norm_g[l, 2]), mod_l[3], mod_l[4])
        x = x + mod_l[5] * _rmsnorm(ffn(h_l), norm_g[l, 3])
        if not last:
            h_c = _modulate(_rmsnorm(xc, norm_g[l, 2]), mod_c[3], mod_c[4])
            xc = xc + mod_c[5] * _rmsnorm(ffn(h_c), norm_g[l, 3])
    return x
```

```python
import functools
import math

import jax
import jax.numpy as jnp
from jax import lax
from jax.experimental import pallas as pl
from jax.experimental.pallas import tpu as pltpu

F32 = jnp.float32
BF16 = jnp.bfloat16

D_MODEL = 1024
HEAD_DIM = 64
N_Q_HEADS = 8
N_KV_HEADS = 2
Q_PER_KV = N_Q_HEADS // N_KV_HEADS
D_ATTN = N_Q_HEADS * HEAD_DIM
D_KV = N_KV_HEADS * HEAD_DIM
D_SSM = 512
SSM_GROUP = 16
N_SSM_GROUPS = D_SSM // SSM_GROUP
SSM_STATE = 64
GRID_W = 64
WINDOW = 128
ATTN_BLOCK = 128
ROPE_THETA = 10000.0
N_EXPERTS = 8
RMS_EPS = 1e-6
NEG_INF = -1e30

SSM_CHUNK = 16
SSM_ROW = SSM_CHUNK * SSM_GROUP
SSM_GROUPS_PER_STEP = 2
TOKEN_TILE = 512
MOE_TILE = 2048
MOE_ROWS = 256
MOE_FF_TILE = 896
FFN_CHUNK = 1408
VMEM_LIMIT = 56 * 1024 * 1024


def _params(sem, vmem=None):
    return pltpu.CompilerParams(dimension_semantics=sem, vmem_limit_bytes=vmem)


def _resident(shape):
    nd = len(shape)
    return pl.BlockSpec(shape, lambda *_: (0,) * nd, pipeline_mode=pl.Buffered(1))


def _norm_modulate(x, g, shift, scale):
    y = x * lax.rsqrt(jnp.mean(x * x, axis=-1, keepdims=True) + RMS_EPS) * g
    return y * (1.0 + scale) + shift


def _rmsnorm(x, g):
    return x * lax.rsqrt(jnp.mean(x * x, axis=-1, keepdims=True) + RMS_EPS) * g


def _sigmoid(x):
    return 1.0 / (1.0 + jnp.exp(-x))


def _silu(x):
    return x * _sigmoid(x)


def _gelu_tanh(x):
    c = math.sqrt(2.0 / math.pi)
    return 0.5 * x * (1.0 + jnp.tanh(c * (x + 0.044715 * (x * x * x))))


def _mod_kernel(c_ref, w_ref, b_ref, o_ref):
    c = c_ref[...]
    o_ref[0] = jnp.dot(_silu(c), w_ref[0], preferred_element_type=F32,
                       precision=lax.Precision.HIGHEST) + b_ref[0]


def _modulation(cond, w_mod, b_mod):
    depth, d, d6 = w_mod.shape
    rows = cond.shape[0]
    nblk = d6 // d
    return pl.pallas_call(
        _mod_kernel,
        grid=(depth, nblk),
        in_specs=[
            pl.BlockSpec((rows, d), lambda l, j: (0, 0)),
            pl.BlockSpec((1, d, d), lambda l, j: (l, 0, j)),
            pl.BlockSpec((1, 1, d), lambda l, j: (l, 0, j)),
        ],
        out_specs=pl.BlockSpec((1, rows, d), lambda l, j: (l, 0, j)),
        out_shape=jax.ShapeDtypeStruct((depth, rows, d6), F32),
        compiler_params=_params(("parallel", "parallel"), VMEM_LIMIT),
        name="adaln_mod",
    )(cond, w_mod, b_mod.reshape(depth, 1, d6))


def _proj_kernel(x_ref, mod_ref, g_ref, w_ref, cos_ref, slo_ref, shi_ref,
                 q_ref, k_ref, v_ref, u_ref, ga_ref, gb_ref):
    h = _norm_modulate(x_ref[...], g_ref[...], mod_ref[0, 0:1, :], mod_ref[0, 1:2, :]).astype(BF16)

    def proj(lo, hi):
        return jnp.dot(h, w_ref[:, lo:hi], preferred_element_type=F32)

    cos, slo, shi = cos_ref[...], slo_ref[...], shi_ref[...]

    def rope(t):
        w = t.shape[1]
        reps = w // cos.shape[1]
        c, lo, hi = (jnp.tile(a, (1, reps)) if reps > 1 else a for a in (cos, slo, shi))
        half = HEAD_DIM // 2
        return t * c + pltpu.roll(t, half, 1) * hi + pltpu.roll(t, w - half, 1) * lo

    o = 0
    q_ref[...] = (rope(proj(o, o + D_ATTN)) * (HEAD_DIM ** -0.5)).astype(BF16)
    o += D_ATTN
    k_ref[...] = rope(proj(o, o + D_KV)).astype(BF16)
    o += D_KV
    v_ref[...] = proj(o, o + D_KV).astype(BF16)
    o += D_KV
    u_ref[...] = proj(o, o + D_SSM)
    o += D_SSM
    ga_ref[...] = proj(o, o + D_MODEL)
    o += D_MODEL
    gb_ref[...] = proj(o, o + D_MODEL)


def _mod_index(i, lat_tiles, tiles_per_batch, n_batch):
    return jnp.where(i < lat_tiles, i // tiles_per_batch, n_batch)


def _proj(x, mod, g, w_in, rope_tabs, *, n_batch, seq):
    n, d = x.shape
    tm = TOKEN_TILE
    lat_tiles = n_batch * seq // tm
    tpb = seq // tm
    mod_map = lambda i: (_mod_index(i, lat_tiles, tpb, n_batch), 0, 0)
    tab_map = lambda i: (jnp.where(i < lat_tiles, i % tpb, tpb), 0)
    row = lambda w: pl.BlockSpec((tm, w), lambda i: (i, 0))
    widths = (D_ATTN, D_KV, D_KV, D_SSM, D_MODEL, D_MODEL)
    dtypes = (BF16, BF16, BF16, F32, F32, F32)
    return pl.pallas_call(
        _proj_kernel,
        grid=(n // tm,),
        in_specs=[
            row(d),
            pl.BlockSpec((1, 6, d), mod_map),
            _resident((1, d)),
            _resident(w_in.shape),
            pl.BlockSpec((tm, 2 * HEAD_DIM), tab_map),
            pl.BlockSpec((tm, 2 * HEAD_DIM), tab_map),
            pl.BlockSpec((tm, 2 * HEAD_DIM), tab_map),
        ],
        out_specs=[row(w) for w in widths],
        out_shape=[jax.ShapeDtypeStruct((n, w), t) for w, t in zip(widths, dtypes)],
        compiler_params=_params(("parallel",), VMEM_LIMIT),
        name="mixer_in_proj",
    )(x, mod, g, w_in, *rope_tabs)


def _rope_tables(seq, pad_rows):
    pos = jnp.arange(seq, dtype=F32)
    rowp, colp = jnp.floor(pos / GRID_W), pos % GRID_W
    n_freq = HEAD_DIM // 4
    inv_freq = ROPE_THETA ** (-jnp.arange(n_freq, dtype=F32) / n_freq)
    ang = jnp.concatenate([rowp[:, None] * inv_freq, colp[:, None] * inv_freq], axis=-1)
    cos, sin = jnp.cos(ang), jnp.sin(ang)
    zero = jnp.zeros_like(sin)
    cos_h = jnp.concatenate([cos, cos], -1)
    slo_h = jnp.concatenate([-sin, zero], -1)
    shi_h = jnp.concatenate([zero, sin], -1)

    def finish(t, fill):
        t = jnp.concatenate([t, t], -1)
        return jnp.concatenate([t, jnp.full((pad_rows, t.shape[1]), fill, F32)], 0)

    return finish(cos_h, 1.0), finish(slo_h, 0.0), finish(shi_h, 0.0)


def _attend(q, k, v, sink_ref, valid):
    rows = q.shape[0]
    rid = lax.broadcasted_iota(jnp.int32, (Q_PER_KV * rows, 1), 0)
    outs = []
    for h in range(N_KV_HEADS):
        kh = k[:, h * HEAD_DIM:(h + 1) * HEAD_DIM]
        vh = v[:, h * HEAD_DIM:(h + 1) * HEAD_DIM]
        heads = [h * Q_PER_KV + g for g in range(Q_PER_KV)]
        qs = jnp.concatenate([q[:, hq * HEAD_DIM:(hq + 1) * HEAD_DIM] for hq in heads], axis=0)
        s = lax.dot_general(qs, kh, (((1,), (1,)), ((), ())), preferred_element_type=F32)
        if valid is not None:
            s = jnp.where(valid, s, NEG_INF)
        sink = jnp.full((Q_PER_KV * rows, 1), sink_ref[heads[-1]], F32)
        for g in range(Q_PER_KV - 2, -1, -1):
            sink = jnp.where(rid < (g + 1) * rows, sink_ref[heads[g]], sink)
        m = jnp.maximum(jnp.max(s, axis=-1, keepdims=True), sink)
        p = jnp.exp(s - m)
        denom = jnp.sum(p, axis=-1, keepdims=True) + jnp.exp(sink - m)
        o = jnp.dot(p.astype(BF16), vh, preferred_element_type=F32) / denom
        outs += [o[g * rows:(g + 1) * rows] for g in range(Q_PER_KV)]
    return jnp.concatenate(outs, axis=1)


def _attn_lat_kernel(sink_ref, q_ref, kp_ref, kc_ref, kn_ref, vp_ref, vc_ref, vn_ref,
                     kx_ref, vx_ref, o_ref, *, seq):
    n = pl.program_id(1)
    blk = ATTN_BLOCK
    k = jnp.concatenate([kp_ref[...], kc_ref[...], kn_ref[...], kx_ref[...]], axis=0)
    v = jnp.concatenate([vp_ref[...], vc_ref[...], vn_ref[...], vx_ref[...]], axis=0)
    nk = k.shape[0]
    shape = (Q_PER_KV * blk, nk)
    qi = lax.broadcasted_iota(jnp.int32, shape, 0) & (blk - 1)
    kj = lax.broadcasted_iota(jnp.int32, shape, 1)
    kpos = (n - 1) * blk + kj
    dist = (n * blk + qi) - kpos
    local = (jnp.abs(dist) <= WINDOW) & (kpos >= 0) & (kpos < seq)
    valid = local | (kj >= 3 * blk)
    o_ref[...] = _attend(q_ref[...], k, v, sink_ref, valid).astype(o_ref.dtype)


def _attn_ctx_kernel(sink_ref, q_ref, k_ref, v_ref, o_ref):
    o_ref[...] = _attend(q_ref[...], k_ref[...], v_ref[...], sink_ref, None).astype(o_ref.dtype)


def _attention(q, k, v, sink, *, n_batch, seq, n_ctx, with_ctx_queries):
    n = q.shape[0]
    blk = ATTN_BLOCK
    nb = seq // blk
    ctx0 = n_batch * seq // n_ctx
    smem = pl.BlockSpec(memory_space=pltpu.SMEM)
    kv_w = D_KV

    def kv_spec(shift):
        return pl.BlockSpec(
            (blk, kv_w), lambda b, i: (b * nb + jnp.clip(i + shift, 0, nb - 1), 0))

    ctx_spec = pl.BlockSpec((n_ctx, kv_w), lambda b, i: (ctx0 + b, 0))
    o_lat = pl.pallas_call(
        functools.partial(_attn_lat_kernel, seq=seq),
        grid=(n_batch, nb),
        in_specs=[smem, pl.BlockSpec((blk, D_ATTN), lambda b, i: (b * nb + i, 0)),
                  kv_spec(-1), kv_spec(0), kv_spec(1), kv_spec(-1), kv_spec(0), kv_spec(1),
                  ctx_spec, ctx_spec],
        out_specs=pl.BlockSpec((blk, D_ATTN), lambda b, i: (b * nb + i, 0)),
        out_shape=jax.ShapeDtypeStruct((n, D_ATTN), BF16),
        compiler_params=_params(("parallel", "parallel"), VMEM_LIMIT),
        name="window_attention",
    )(sink, q, k, k, k, v, v, v, k, v)
    if not with_ctx_queries:
        return o_lat, None
    o_ctx = pl.pallas_call(
        _attn_ctx_kernel,
        grid=(n_batch,),
        in_specs=[smem, pl.BlockSpec((n_ctx, D_ATTN), lambda b: (ctx0 + b, 0)),
                  pl.BlockSpec((n_ctx, kv_w), lambda b: (ctx0 + b, 0)),
                  pl.BlockSpec((n_ctx, kv_w), lambda b: (ctx0 + b, 0))],
        out_specs=pl.BlockSpec((n_ctx, D_ATTN), lambda b: (b, 0)),
        out_shape=jax.ShapeDtypeStruct((n_batch * n_ctx, D_ATTN), BF16),
        compiler_params=_params(("parallel",), VMEM_LIMIT),
        name="context_attention",
    )(sink, q, k, v)
    return o_lat, o_ctx


def _ssm_matrices(a_re, a_im, log_dt, b_re, b_im, c_re, c_im):
    hp = lax.Precision.HIGHEST
    g_n, p_n, h_n, ch = N_SSM_GROUPS, SSM_STATE, SSM_GROUP, SSM_CHUNK
    dt = jnp.exp(log_dt.astype(F32))[..., None]
    are, aim = a_re.astype(F32), a_im.astype(F32)
    zr, zi = are * dt, aim * dt
    npow = jnp.arange(ch + 1, dtype=F32)[:, None, None, None]
    mag = jnp.exp(npow * zr)
    pr, pi = mag * jnp.cos(npow * zi), mag * jnp.sin(npow * zi)
    nr, ni = pr[1] - 1.0, pi[1]
    den = are * are + aim * aim
    fr, fi = (nr * are + ni * aim) / den, (ni * are - nr * aim) / den
    bre, bim = b_re.astype(F32), b_im.astype(F32)
    bbr = fr[..., None] * bre - fi[..., None] * bim
    bbi = fr[..., None] * bim + fi[..., None] * bre
    cre, cim = c_re.astype(F32), c_im.astype(F32)
    er = cre[None] * pr[:, :, :, None, :] - cim[None] * pi[:, :, :, None, :]
    ei = cre[None] * pi[:, :, :, None, :] + cim[None] * pr[:, :, :, None, :]
    kern = (jnp.einsum('ndghp,dgpk->ndghk', er, bbr, precision=hp)
            - jnp.einsum('ndghp,dgpk->ndghk', ei, bbi, precision=hp))
    kf, kb = kern[:ch, 0], kern[:ch, 1]
    lags = jnp.concatenate([kb[:0:-1], (kf[0] + kb[0])[None], kf[1:]], axis=0)
    i_idx = jnp.arange(ch)
    toep = lags[i_idx[None, :] - i_idx[:, None] + (ch - 1)]
    wt = toep.transpose(2, 0, 4, 1, 3).reshape(g_n, ch * h_n, ch * h_n)

    def local_state(d, powers):
        ar, ai = pr[powers, d], pi[powers, d]
        lr = ar[..., None] * bbr[d][None] - ai[..., None] * bbi[d][None]
        li = ar[..., None] * bbi[d][None] + ai[..., None] * bbr[d][None]
        lr = lr.transpose(1, 0, 3, 2).reshape(g_n, ch * h_n, p_n)
        li = li.transpose(1, 0, 3, 2).reshape(g_n, ch * h_n, p_n)
        return jnp.concatenate([lr, li, li, lr], axis=-1)

    w1 = jnp.concatenate([wt, local_state(0, (ch - 1) - i_idx), local_state(1, i_idx)], axis=-1)

    def readout(d, powers):
        mr, mi = er[powers, d], ei[powers, d]
        mr = mr.transpose(1, 3, 0, 2).reshape(g_n, p_n, ch * h_n)
        mi = mi.transpose(1, 3, 0, 2).reshape(g_n, p_n, ch * h_n)
        return jnp.concatenate([mr, -mi], axis=1)

    wc = jnp.concatenate([readout(0, i_idx + 1), readout(1, ch - i_idx)], axis=1)
    ar16, ai16 = pr[ch], pi[ch]
    coef = jnp.stack([jnp.concatenate([ar16[0], ar16[0]], -1), jnp.concatenate([-ai16[0], ai16[0]], -1),
                      jnp.concatenate([ar16[1], ar16[1]], -1), jnp.concatenate([-ai16[1], ai16[1]], -1)],
                     axis=1)
    return w1.astype(BF16), wc.astype(BF16), coef


def _ssm_kernel(u_ref, w1_ref, wc_ref, coef_ref, y_ref, l_scr, x_scr, *, n_ctx_chunks, n_chunks, bsz):
    gb = SSM_GROUPS_PER_STEP
    row, st = SSM_ROW, 2 * SSM_STATE
    for gi in range(gb):
        u = u_ref[:, gi * row:(gi + 1) * row]
        y_ref[:, gi * row:(gi + 1) * row] = jnp.dot(u, w1_ref[gi, :, :row], preferred_element_type=F32)
        l_scr[:, gi * 4 * st:(gi + 1) * 4 * st] = jnp.dot(u, w1_ref[gi, :, row:], preferred_element_type=F32)

    coefs = [[jnp.broadcast_to(coef_ref[gi, k:k + 1, :], (bsz, st)) for k in range(4)] for gi in range(gb)]

    def step(s, carry):
        cb = jnp.where(s < n_ctx_chunks, n_ctx_chunks - 1 - s, n_chunks + n_ctx_chunks - 1 - s)
        rf = pl.multiple_of(s * bsz, bsz)
        rb = pl.multiple_of(cb * bsz, bsz)
        out = []
        for gi in range(gb):
            a1f, a2f, a1b, a2b = coefs[gi]
            v1f, v2f, v1b, v2b = carry[gi]
            x_scr[pl.ds(rf, bsz), gi * 2 * st:gi * 2 * st + st] = v1f
            x_scr[pl.ds(rb, bsz), gi * 2 * st + st:(gi + 1) * 2 * st] = v1b
            base = gi * 4 * st
            l1f = l_scr[pl.ds(rf, bsz), base:base + st]
            l2f = l_scr[pl.ds(rf, bsz), base + st:base + 2 * st]
            l1b = l_scr[pl.ds(rb, bsz), base + 2 * st:base + 3 * st]
            l2b = l_scr[pl.ds(rb, bsz), base + 3 * st:base + 4 * st]
            out.append((a1f * v1f + a2f * v2f + l1f, a1f * v2f - a2f * v1f + l2f,
                        a1b * v1b + a2b * v2b + l1b, a1b * v2b - a2b * v1b + l2b))
        return tuple(out)

    zero = jnp.zeros((bsz, st), F32)
    lax.fori_loop(0, n_chunks, step, tuple((zero,) * 4 for _ in range(gb)))
    for gi in range(gb):
        xs = x_scr[:, gi * 2 * st:(gi + 1) * 2 * st].astype(BF16)
        y_ref[:, gi * row:(gi + 1) * row] += jnp.dot(xs, wc_ref[gi], preferred_element_type=F32)


def _ssm(u_rows, w1, wc, coef, *, n_ctx_chunks, n_chunks, bsz):
    r, width = u_rows.shape
    gb = SSM_GROUPS_PER_STEP
    row, st = SSM_ROW, 2 * SSM_STATE
    return pl.pallas_call(
        functools.partial(_ssm_kernel, n_ctx_chunks=n_ctx_chunks, n_chunks=n_chunks, bsz=bsz),
        grid=(N_SSM_GROUPS // gb,),
        in_specs=[
            pl.BlockSpec((r, gb * row), lambda g: (0, g)),
            pl.BlockSpec((gb, row, row + 4 * st), lambda g: (g, 0, 0)),
            pl.BlockSpec((gb, 2 * st, row), lambda g: (g, 0, 0)),
            pl.BlockSpec((gb, 4, st), lambda g: (g, 0, 0)),
        ],
        out_specs=pl.BlockSpec((r, gb * row), lambda g: (0, g)),
        out_shape=jax.ShapeDtypeStruct((r, width), F32),
        scratch_shapes=[pltpu.VMEM((r, gb * 4 * st), F32), pltpu.VMEM((r, gb * 2 * st), F32)],
        compiler_params=_params(("parallel",), VMEM_LIMIT),
        name="s5_chunked_scan",
    )(u_rows, w1, wc, coef)


def _to_chunk_rows(u, *, n_batch, seq, n_ctx):
    g, h, ch = N_SSM_GROUPS, SSM_GROUP, SSM_CHUNK

    def part(t, length):
        t = t.reshape(n_batch, length // ch, ch, g, h).transpose(1, 0, 3, 2, 4)
        return t.reshape(length // ch * n_batch, g * ch * h)

    lat, ctx = u[:n_batch * seq], u[n_batch * seq:]
    return jnp.concatenate([part(ctx, n_ctx), part(lat, seq)], axis=0).astype(BF16)


def _from_chunk_rows(y, *, n_batch, seq, n_ctx):
    g, h, ch = N_SSM_GROUPS, SSM_GROUP, SSM_CHUNK

    def part(t, length):
        t = t.reshape(length // ch, n_batch, g, ch, h).transpose(1, 0, 3, 2, 4)
        return t.reshape(n_batch * length, g * h)

    n_ctx_rows = n_ctx // ch * n_batch
    return jnp.concatenate([part(y[n_ctx_rows:], seq), part(y[:n_ctx_rows], n_ctx)], axis=0)


def _merge_kernel(x_ref, mod_ref, o_ref, y_ref, u_ref, ga_ref, gb_ref, d_ref, wglu_ref, bglu_ref,
                  wpa_ref, wpb_ref, wo_ref, g_ref, out_ref):
    y = y_ref[...] + d_ref[...] * u_ref[...]
    gl = _gelu_tanh(y)
    s = gl * _sigmoid(jnp.dot(gl.astype(BF16), wglu_ref[...], preferred_element_type=F32) + bglu_ref[...])
    pa = jnp.dot(o_ref[...], wpa_ref[...], preferred_element_type=F32)
    pb = jnp.dot(s.astype(BF16), wpb_ref[...], preferred_element_type=F32)
    m = _sigmoid(ga_ref[...]) * pa + _sigmoid(gb_ref[...]) * pb
    mo = jnp.dot(m.astype(BF16), wo_ref[...], preferred_element_type=F32)
    out_ref[...] = x_ref[...] + mod_ref[0, 2:3, :] * _rmsnorm(mo, g_ref[...])


def _merge(x, mod, o, y, u, ga, gb, d_skip, w_glu, b_glu, w_pa, w_pb, w_o, g, *, n_rows, n_batch, seq):
    d = x.shape[1]
    tm = TOKEN_TILE
    lat_tiles, tpb = n_batch * seq // tm, seq // tm
    mod_map = lambda i: (_mod_index(i, lat_tiles, tpb, n_batch), 0, 0)
    row = lambda w: pl.BlockSpec((tm, w), lambda i: (i, 0))
    return pl.pallas_call(
        _merge_kernel,
        grid=(n_rows // tm,),
        in_specs=[row(d), pl.BlockSpec((1, 6, d), mod_map), row(D_ATTN), row(D_SSM), row(D_SSM),
                  row(d), row(d), _resident((1, D_SSM)), _resident(w_glu.shape), _resident((1, D_SSM)),
                  _resident(w_pa.shape), _resident(w_pb.shape), _resident(w_o.shape), _resident((1, d))],
        out_specs=row(d),
        out_shape=jax.ShapeDtypeStruct((n_rows, d), F32),
        compiler_params=_params(("parallel",), VMEM_LIMIT),
        name="mixer_merge",
    )(x, mod, o, y, u, ga, gb, d_skip, w_glu, b_glu, w_pa, w_pb, w_o, g)


def _ffn_kernel(x_ref, mod_ref, gin_ref, wg_ref, wu_ref, wd_ref, gout_ref, out_ref):
    x = x_ref[...]
    h = _norm_modulate(x, gin_ref[...], mod_ref[0, 3:4, :], mod_ref[0, 4:5, :]).astype(BF16)
    d_ff = wg_ref.shape[1]
    acc = None
    for lo in range(0, d_ff, FFN_CHUNK):
        hi = lo + FFN_CHUNK
        a = _silu(jnp.dot(h, wg_ref[:, lo:hi], preferred_element_type=F32)) * jnp.dot(
            h, wu_ref[:, lo:hi], preferred_element_type=F32)
        part = jnp.dot(a.astype(BF16), wd_ref[lo:hi, :], preferred_element_type=F32)
        acc = part if acc is None else acc + part
    out_ref[...] = x + mod_ref[0, 5:6, :] * _rmsnorm(acc, gout_ref[...])


def _ffn(x, mod, g_in, wg, wu, wd, g_out, *, n_rows, n_batch, seq):
    d = x.shape[1]
    tm = TOKEN_TILE
    lat_tiles, tpb = n_batch * seq // tm, seq // tm
    mod_map = lambda i: (_mod_index(i, lat_tiles, tpb, n_batch), 0, 0)
    row = pl.BlockSpec((tm, d), lambda i: (i, 0))
    return pl.pallas_call(
        _ffn_kernel,
        grid=(n_rows // tm,),
        in_specs=[row, pl.BlockSpec((1, 6, d), mod_map), _resident((1, d)), _resident(wg.shape),
                  _resident(wu.shape), _resident(wd.shape), _resident((1, d))],
        out_specs=row,
        out_shape=jax.ShapeDtypeStruct((n_rows, d), F32),
        compiler_params=_params(("parallel",), VMEM_LIMIT),
        name="dense_swiglu",
    )(x, mod, g_in, wg, wu, wd, g_out)


def _router_kernel(x_ref, mod_ref, gin_ref, wr_ref, br_ref, h_ref, gate_ref, pos_ref, cnt_ref):
    h = _norm_modulate(x_ref[...], gin_ref[...], mod_ref[0, 3:4, :], mod_ref[0, 4:5, :])
    h_ref[...] = h.astype(BF16)
    logits = lax.dot_general(wr_ref[...], h, (((1,), (1,)), ((), ())), preferred_element_type=F32,
                             precision=lax.Precision.HIGHEST) + br_ref[...]
    ne, tile = logits.shape
    eid = lax.broadcasted_iota(jnp.int32, (ne, tile), 0)
    m1 = jnp.max(logits, axis=0, keepdims=True)
    i1 = jnp.min(jnp.where(logits == m1, eid, ne), axis=0, keepdims=True)
    rest = jnp.where(eid == i1, -jnp.inf, logits)
    m2 = jnp.max(rest, axis=0, keepdims=True)
    i2 = jnp.min(jnp.where(rest == m2, eid, ne), axis=0, keepdims=True)
    e2 = jnp.exp(m2 - m1)
    w1, w2 = 1.0 / (1.0 + e2), e2 / (1.0 + e2)
    gate_ref[...] = jnp.where(eid == i1, w1, 0.0) + jnp.where(eid == i2, w2, 0.0)
    routed = (eid == i1) | (eid == i2)
    blk = 256
    tri = (lax.broadcasted_iota(jnp.int32, (blk, blk), 0) < lax.broadcasted_iota(jnp.int32, (blk, blk), 1)
           ).astype(BF16)
    mask_bf = routed.astype(BF16)
    total = jnp.zeros((ne, 1), F32)
    for b0 in range(0, tile, blk):
        mb = mask_bf[:, b0:b0 + blk]
        within = jnp.dot(mb, tri, preferred_element_type=F32) + total
        pos_ref[:, b0:b0 + blk] = jnp.where(routed[:, b0:b0 + blk], within.astype(jnp.int32), -1)
        total = total + jnp.sum(mb.astype(F32), axis=1, keepdims=True)
    cnt_ref[0] = jnp.broadcast_to(total.astype(jnp.int32), (ne, cnt_ref.shape[2]))


def _router(x, mod, g_in, w_router_t, b_router, *, n_rows, n_batch, seq):
    d = x.shape[1]
    tm = MOE_TILE
    lat_tiles, tpb = n_batch * seq // tm, seq // tm
    mod_map = lambda i: (_mod_index(i, lat_tiles, tpb, n_batch), 0, 0)
    ne = N_EXPERTS
    nt = n_rows // tm
    return pl.pallas_call(
        _router_kernel,
        grid=(nt,),
        in_specs=[pl.BlockSpec((tm, d), lambda i: (i, 0)), pl.BlockSpec((1, 6, d), mod_map),
                  _resident((1, d)), _resident((ne, d)), _resident((ne, 1))],
        out_specs=[pl.BlockSpec((tm, d), lambda i: (i, 0)), pl.BlockSpec((ne, tm), lambda i: (0, i)),
                   pl.BlockSpec((ne, tm), lambda i: (0, i)), pl.BlockSpec((1, ne, 128), lambda i: (i, 0, 0))],
        out_shape=[jax.ShapeDtypeStruct((n_rows, d), BF16), jax.ShapeDtypeStruct((ne, n_rows), F32),
                   jax.ShapeDtypeStruct((ne, n_rows), jnp.int32),
                   jax.ShapeDtypeStruct((nt, ne, 128), jnp.int32)],
        compiler_params=_params(("parallel",), VMEM_LIMIT),
        name="moe_router",
    )(x, mod, g_in, w_router_t, b_router)


def _expert_kernel(nch_ref, h_ref, pos_ref, gate_ref, wg_ref, wu_ref, wd_ref, y_ref, xe_scr, ye_scr):
    i, e, f = pl.program_id(0), pl.program_id(1), pl.program_id(2)
    n_chunks = nch_ref[i * N_EXPERTS + e]
    rows = MOE_ROWS
    tile = h_ref.shape[0]

    def slots(c):
        pos = pos_ref[pl.ds(e, 1), :]
        rid = lax.broadcasted_iota(jnp.int32, (rows, tile), 0) + c * rows
        return pos == rid

    @pl.when(f == 0)
    def _gather():
        def body(c, _):
            r0 = pl.multiple_of(c * rows, rows)
            xe_scr[pl.ds(r0, rows), :] = jnp.dot(
                slots(c).astype(BF16), h_ref[...], preferred_element_type=F32).astype(BF16)
            return 0
        lax.fori_loop(0, n_chunks, body, 0)

    def ffn_body(c, _):
        r0 = pl.multiple_of(c * rows, rows)
        xr = xe_scr[pl.ds(r0, rows), :]
        a = _silu(jnp.dot(xr, wg_ref[0], preferred_element_type=F32)) * jnp.dot(
            xr, wu_ref[0], preferred_element_type=F32)
        part = jnp.dot(a.astype(BF16), wd_ref[0], preferred_element_type=F32)

        @pl.when(f == 0)
        def _first():
            ye_scr[pl.ds(r0, rows), :] = part

        @pl.when(f != 0)
        def _accumulate():
            ye_scr[pl.ds(r0, rows), :] += part
        return 0
    lax.fori_loop(0, n_chunks, ffn_body, 0)

    @pl.when(f == pl.num_programs(2) - 1)
    def _combine():
        @pl.when(e == 0)
        def _zero():
            y_ref[...] = jnp.zeros_like(y_ref)

        def body(c, _):
            r0 = pl.multiple_of(c * rows, rows)
            sel = slots(c)
            gate = jnp.sum(jnp.where(sel, gate_ref[pl.ds(e, 1), :], 0.0), axis=1, keepdims=True)
            ys = (ye_scr[pl.ds(r0, rows), :] * gate).astype(BF16)
            y_ref[...] += lax.dot_general(sel.astype(BF16), ys, (((0,), (0,)), ((), ())),
                                          preferred_element_type=F32)
            return 0
        lax.fori_loop(0, n_chunks, body, 0)


def _experts(h, pos, gate, n_chunks, wg, wu, wd):
    n_rows, d = h.shape
    tm = MOE_TILE
    ne, _, d_ff = wg.shape
    tf = MOE_FF_TILE
    grid_spec = pltpu.PrefetchScalarGridSpec(
        num_scalar_prefetch=1,
        grid=(n_rows // tm, ne, d_ff // tf),
        in_specs=[
            pl.BlockSpec((tm, d), lambda i, e, f, n: (i, 0)),
            pl.BlockSpec((ne, tm), lambda i, e, f, n: (0, i)),
            pl.BlockSpec((ne, tm), lambda i, e, f, n: (0, i)),
            pl.BlockSpec((1, d, tf), lambda i, e, f, n: (e, 0, f)),
            pl.BlockSpec((1, d, tf), lambda i, e, f, n: (e, 0, f)),
            pl.BlockSpec((1, tf, d), lambda i, e, f, n: (e, f, 0)),
        ],
        out_specs=pl.BlockSpec((tm, d), lambda i, e, f, n: (i, 0)),
        scratch_shapes=[pltpu.VMEM((tm, d), BF16), pltpu.VMEM((tm, d), F32)],
    )
    return pl.pallas_call(
        _expert_kernel,
        grid_spec=grid_spec,
        out_shape=jax.ShapeDtypeStruct((n_rows, d), F32),
        compiler_params=_params(("parallel", "arbitrary", "arbitrary"), VMEM_LIMIT),
        name="moe_experts",
    )(n_chunks, h, pos, gate, wg, wu, wd)


def _residual_kernel(x_ref, y_ref, mod_ref, g_ref, out_ref):
    out_ref[...] = x_ref[...] + mod_ref[0, 5:6, :] * _rmsnorm(y_ref[...], g_ref[...])


def _residual(x, y, mod, g_out, *, n_rows, n_batch, seq):
    d = x.shape[1]
    tm = TOKEN_TILE
    lat_tiles, tpb = n_batch * seq // tm, seq // tm
    mod_map = lambda i: (_mod_index(i, lat_tiles, tpb, n_batch), 0, 0)
    row = pl.BlockSpec((tm, d), lambda i: (i, 0))
    return pl.pallas_call(
        _residual_kernel,
        grid=(n_rows // tm,),
        in_specs=[row, row, pl.BlockSpec((1, 6, d), mod_map), _resident((1, d))],
        out_specs=row,
        out_shape=jax.ShapeDtypeStruct((n_rows, d), F32),
        compiler_params=_params(("parallel",), VMEM_LIMIT),
        name="moe_residual",
    )(x, y, mod, g_out)


def _moe(x, mod, g_in, w_router, b_router, wg, wu, wd, g_out, *, n_rows, n_batch, seq):
    h, gate, pos, cnt = _router(x, mod, g_in, w_router.T.astype(F32), b_router.reshape(-1, 1).astype(F32),
                                n_rows=n_rows, n_batch=n_batch, seq=seq)
    n_chunks = ((cnt[:, :, 0] + (MOE_ROWS - 1)) // MOE_ROWS).reshape(-1)
    y = _experts(h, pos, gate, n_chunks, wg, wu, wd)
    return _residual(x, y, mod, g_out, n_rows=n_rows, n_batch=n_batch, seq=seq)


def kernel(x, c, ctx, c_ctx, w_mod, b_mod, norm_g, w_in, attn_sink, ssm_a_re, ssm_a_im, ssm_log_dt,
           ssm_b_re, ssm_b_im, ssm_c_re, ssm_c_im, ssm_d, w_glu, b_glu, w_branch_attn, w_branch_ssm,
           w_out, w_ffn_gate, w_ffn_up, w_ffn_down, w_router, b_router, w_exp_gate, w_exp_up,
           w_exp_down):
    n_batch, seq, d = x.shape
    n_ctx = ctx.shape[1]
    depth = w_mod.shape[0]
    n_lat, n_all = n_batch * seq, n_batch * (seq + n_ctx)
    assert d == D_MODEL and n_batch == 8, "batch rides the 8 sublanes of the S5 chunk scan"
    assert seq % TOKEN_TILE == 0 and (n_batch * n_ctx) % TOKEN_TILE == 0
    assert n_lat % MOE_TILE == 0 and n_all % MOE_TILE == 0 and seq % MOE_TILE == 0
    assert seq % GRID_W == 0 and seq % ATTN_BLOCK == 0 and n_lat % n_ctx == 0
    assert seq % SSM_CHUNK == 0 and n_ctx % SSM_CHUNK == 0

    cond_rows = 16
    cond = jnp.concatenate([c.astype(F32), c_ctx.astype(F32)[None],
                            jnp.zeros((cond_rows - n_batch - 1, d), F32)], axis=0)
    mod_all = _modulation(cond, w_mod.astype(F32), b_mod.astype(F32))
    mod_all = mod_all.reshape(depth, cond_rows, 6, d)[:, :n_batch + 1]

    rope_tabs = _rope_tables(seq, TOKEN_TILE)
    xs = jnp.concatenate([x.reshape(n_lat, d), ctx.reshape(n_batch * n_ctx, d)], axis=0).astype(F32)
    n_ctx_chunks, n_chunks = n_ctx // SSM_CHUNK, (n_ctx + seq) // SSM_CHUNK
    geom = dict(n_batch=n_batch, seq=seq)

    for l in range(depth):
        last = l == depth - 1
        n_rows = n_lat if last else n_all
        mod = mod_all[l]
        g = norm_g[l].astype(F32).reshape(4, 1, d)

        q, k, v, u, ga, gb = _proj(xs, mod, g[0], w_in[l].astype(BF16), rope_tabs, **geom)
        o_lat, o_ctx = _attention(q, k, v, attn_sink[l].astype(F32), n_ctx=n_ctx,
                                  with_ctx_queries=not last, **geom)
        o = o_lat if last else jnp.concatenate([o_lat[:n_lat], o_ctx], axis=0)

        w1, wc, coef = _ssm_matrices(ssm_a_re[l], ssm_a_im[l], ssm_log_dt[l], ssm_b_re[l], ssm_b_im[l],
                                     ssm_c_re[l], ssm_c_im[l])
        y_rows = _ssm(_to_chunk_rows(u, n_ctx=n_ctx, **geom), w1, wc, coef,
                      n_ctx_chunks=n_ctx_chunks, n_chunks=n_chunks, bsz=n_batch)
        y = _from_chunk_rows(y_rows, n_ctx=n_ctx, **geom)

        xs = _merge(xs, mod, o, y, u, ga, gb, ssm_d[l].astype(F32).reshape(1, -1),
                    w_glu[l].astype(BF16), b_glu[l].astype(F32).reshape(1, -1),
                    w_branch_attn[l].astype(BF16), w_branch_ssm[l].astype(BF16), w_out[l].astype(BF16),
                    g[1], n_rows=n_rows, **geom)

        j = l // 2
        if l % 2 == 0:
            xs = _ffn(xs, mod, g[2], w_ffn_gate[j].astype(BF16), w_ffn_up[j].astype(BF16),
                      w_ffn_down[j].astype(BF16), g[3], n_rows=n_rows, **geom)
        else:
            xs = _moe(xs, mod, g[2], w_router[j], b_router[j], w_exp_gate[j].astype(BF16),
                      w_exp_up[j].astype(BF16), w_exp_down[j].astype(BF16), g[3], n_rows=n_rows, **geom)
    return xs[:n_lat].reshape(n_batch, seq, d).astype(x.dtype)
```
